```python
import math
import jax, jax.numpy as jnp
from jax import lax
import numpy as np

D_MODEL = 2048
BATCH = 1
SEQ = 8192
DEPTH = 1
DEC_BATCH = 16
DEC_SEQ = 32
PAST_LEN = 4096

CHUNK = 64
Q_BLOCK = 128
SB_HEADS = 16
SB_HEAD_DIM = 128
SB_WIDTH = SB_HEADS * SB_HEAD_DIM
SSD_EXPAND = 2
SSD_WIDTH = SSD_EXPAND * D_MODEL
SSD_HEAD_DIM = 64
SSD_HEADS = SSD_WIDTH // SSD_HEAD_DIM
SSD_GROUPS = 8
SSD_HEADS_PER_GROUP = SSD_HEADS // SSD_GROUPS
SSD_GROUP_WIDTH = SSD_WIDTH // SSD_GROUPS
SSD_STATE = 128
SSD_CONV = 4
SSD_CONV_DIM = SSD_WIDTH + 2 * SSD_GROUPS * SSD_STATE
SSD_CHUNK = CHUNK
IN_DIM = 3 * SB_WIDTH + SSD_WIDTH + SSD_CONV_DIM + SSD_HEADS + 2 * D_MODEL
N_EXPERTS = 256
TOP_K = 8
N_EXPERT_GROUPS = 8
TOPK_GROUPS = 4
EXPERT_FF = 512
SHARED_FF = 512
ROUTED_SCALE = 2.5
EXPERT_BLOCK = 64
PLE_DIM = 256
DEEPNORM_ALPHA = (2 * DEPTH) ** 0.25
DEEPNORM_BETA = (8 * DEPTH) ** -0.25
LN_EPS = 1e-5
RMS_EPS = 1e-5

kernel_name = "stickbreak_ssd_moe_streaming_step"


def layer_norm(x, g, b):
    xf = x.astype(jnp.float32)
    mu = jnp.mean(xf, -1, keepdims=True)
    var = jnp.mean(jnp.square(xf - mu), -1, keepdims=True)
    return ((xf - mu) * lax.rsqrt(var + LN_EPS) * g.astype(jnp.float32) + b.astype(jnp.float32)).astype(x.dtype)


def stick_breaking(q, k, v, q_pos):
    f32 = jnp.float32
    z = jnp.einsum("bqhd,bkhd->bhqk", q.astype(f32), k.astype(f32)) * (SB_HEAD_DIM ** -0.5)
    reach = jnp.arange(k.shape[1])[None, :] < q_pos[:, None]
    log_keep = jnp.where(reach, jax.nn.log_sigmoid(-z), 0.0)
    later = lax.cumsum(log_keep, axis=3, reverse=True) - log_keep
    w = jnp.where(reach, jnp.exp(jax.nn.log_sigmoid(z) + later), 0.0)
    return jnp.einsum("bhqk,bkhd->bqhd", w, v.astype(f32))


def stick_breaking_blocked(q, k, v):
    b, l, h, d = q.shape
    nb = l // Q_BLOCK
    qb = jnp.swapaxes(q.reshape(b, nb, Q_BLOCK, h, d), 0, 1)
    pos = jnp.arange(l).reshape(nb, Q_BLOCK)
    ob = lax.map(lambda a: stick_breaking(a[0], k, v, a[1]), (qb, pos))
    return jnp.swapaxes(ob, 0, 1).reshape(b, l, h, d)


def causal_conv(x, past, w, bias):
    xp = jnp.concatenate([past.astype(x.dtype), x], axis=1)
    y = lax.conv_general_dilated(xp, w[:, None, :].astype(x.dtype), window_strides=(1,), padding="VALID",
                                 dimension_numbers=("NWC", "WIO", "NWC"), feature_group_count=x.shape[-1])
    return y + bias, xp[:, -(SSD_CONV - 1):]


def ssd_scan(xdt, a, bm, cm, state0, chunk):
    b, l, g, r, p = xdt.shape
    c = l // chunk

    def to_chunks(t):
        return jnp.moveaxis(t.reshape((b, c, chunk) + t.shape[2:]), 1, 0)

    causal = jnp.tril(jnp.ones((chunk, chunk), bool))

    def step(state, inp):
        xc, ac, bc, cc = inp
        acum = jnp.cumsum(ac, axis=1)
        seg = acum[:, :, None] - acum[:, None, :]
        decay = jnp.exp(jnp.where(causal[None, :, :, None, None], seg, -jnp.inf))
        scores = jnp.einsum("blgn,bsgn->blsg", cc, bc)[..., None] * decay
        y = jnp.einsum("blsgr,bsgrp->blgrp", scores, xc)
        y = y + jnp.einsum("blgn,bgrpn->blgrp", cc, state) * jnp.exp(acum)[..., None]
        to_end = jnp.exp(acum[:, -1:] - acum)
        state = state * jnp.exp(acum[:, -1])[..., None, None] + jnp.einsum("blgn,blgrp->bgrpn", bc, xc * to_end[..., None])
        return state, y

    final, ys = lax.scan(step, state0, (to_chunks(xdt), to_chunks(a), to_chunks(bm), to_chunks(cm)))
    return jnp.moveaxis(ys, 0, 1).reshape(b, l, g, r, p), final


def token_mixers(h, w_in, b_gate, conv_w, conv_b, dt_bias, a_log, d_skip, norm_w, w_pa, w_pb, w_o,
                 kv_past, conv_past, ssm_past):
    b, l, _ = h.shape
    f32 = jnp.float32
    cuts = [SB_WIDTH, 2 * SB_WIDTH, 3 * SB_WIDTH, 3 * SB_WIDTH + SSD_WIDTH,
            3 * SB_WIDTH + SSD_WIDTH + SSD_CONV_DIM, 3 * SB_WIDTH + SSD_WIDTH + SSD_CONV_DIM + SSD_HEADS]
    q, k, v, z, xbc, dt, gates = jnp.split(h @ w_in, cuts, axis=-1)
    q = q.reshape(b, l, SB_HEADS, SB_HEAD_DIM)
    k = k.reshape(b, l, SB_HEADS, SB_HEAD_DIM)
    v = v.reshape(b, l, SB_HEADS, SB_HEAD_DIM)
    if kv_past is None:
        attn = stick_breaking_blocked(q, k, v)
    else:
        k_past, v_past = kv_past
        attn = stick_breaking(q, jnp.concatenate([k_past.astype(k.dtype), k], 1),
                              jnp.concatenate([v_past.astype(v.dtype), v], 1), k_past.shape[1] + jnp.arange(l))
    attn = attn.reshape(b, l, SB_WIDTH).astype(h.dtype)
    xbc, conv_state = causal_conv(xbc, conv_past, conv_w, conv_b)
    xbc = jax.nn.silu(xbc).astype(f32)
    xs, bm, cm = jnp.split(xbc, [SSD_WIDTH, SSD_WIDTH + SSD_GROUPS * SSD_STATE], axis=-1)
    xs = xs.reshape(b, l, SSD_GROUPS, SSD_HEADS_PER_GROUP, SSD_HEAD_DIM)
    bm = bm.reshape(b, l, SSD_GROUPS, SSD_STATE)
    cm = cm.reshape(b, l, SSD_GROUPS, SSD_STATE)
    dt = jax.nn.softplus(dt.astype(f32) + dt_bias.astype(f32)).reshape(b, l, SSD_GROUPS, SSD_HEADS_PER_GROUP)
    log_decay = -jnp.exp(a_log.astype(f32)).reshape(SSD_GROUPS, SSD_HEADS_PER_GROUP) * dt
    state0 = ssm_past.astype(f32).reshape(b, SSD_GROUPS, SSD_HEADS_PER_GROUP, SSD_HEAD_DIM, SSD_STATE)
    chunk = SSD_CHUNK if l % SSD_CHUNK == 0 else l
    y, ssm_state = ssd_scan(xs * dt[..., None], log_decay, bm, cm, state0, chunk)
    y = y + d_skip.astype(f32).reshape(SSD_GROUPS, SSD_HEADS_PER_GROUP, 1) * xs
    y = y.reshape(b, l, SSD_GROUPS, SSD_GROUP_WIDTH) * jax.nn.silu(z.astype(f32)).reshape(b, l, SSD_GROUPS, SSD_GROUP_WIDTH)
    y = y * lax.rsqrt(jnp.mean(jnp.square(y), -1, keepdims=True) + RMS_EPS)
    y = (y.reshape(b, l, SSD_WIDTH) * norm_w.astype(f32)).astype(h.dtype)
    gate_a, gate_b = jnp.split(jax.nn.sigmoid(gates + b_gate), 2, axis=-1)
    merged = gate_a * (attn @ w_pa) + gate_b * (y @ w_pb)
    new_ssm = ssm_state.reshape(b, SSD_HEADS, SSD_HEAD_DIM, SSD_STATE).astype(h.dtype)
    return merged @ w_o, k, v, conv_state, new_ssm


def swiglu(x, wg, wu, wd):
    return (jax.nn.silu(x @ wg) * (x @ wu)) @ wd


def route(x, w_router, router_bias):
    t = x.shape[0]
    scores = jax.nn.sigmoid((x @ w_router).astype(jnp.float32))
    choice = scores + router_bias.astype(jnp.float32)
    grouped = choice.reshape(t, N_EXPERT_GROUPS, N_EXPERTS // N_EXPERT_GROUPS)
    group_score = jnp.sum(lax.top_k(grouped, 2)[0], -1)
    _, gidx = lax.top_k(group_score, TOPK_GROUPS)
    gmask = jnp.sum(jax.nn.one_hot(gidx, N_EXPERT_GROUPS), -2) > 0
    emask = jnp.repeat(gmask, N_EXPERTS // N_EXPERT_GROUPS, axis=-1)
    _, idx = lax.top_k(jnp.where(emask, choice, -jnp.inf), TOP_K)
    w = jnp.take_along_axis(scores, idx, -1)
    return idx, w / jnp.sum(w, -1, keepdims=True) * ROUTED_SCALE


def routed_experts(x, idx, gate, w_eg, w_eu, w_ed):
    t, d = x.shape
    n = t * TOP_K
    flat_e = idx.reshape(n)
    flat_tok = jnp.arange(n) // TOP_K
    flat_gate = gate.reshape(n)
    order = jnp.argsort(flat_e)
    e_sorted = flat_e[order]
    counts = jnp.bincount(flat_e, length=N_EXPERTS)
    padded = (counts + EXPERT_BLOCK - 1) // EXPERT_BLOCK * EXPERT_BLOCK
    start = jnp.cumsum(counts) - counts
    pend = jnp.cumsum(padded)
    pstart = pend - padded
    dest = pstart[e_sorted] + jnp.arange(n) - start[e_sorted]
    n_blocks = (n + N_EXPERTS * (EXPERT_BLOCK - 1) + EXPERT_BLOCK - 1) // EXPERT_BLOCK
    npad = n_blocks * EXPERT_BLOCK
    tok_buf = jnp.full((npad,), t, jnp.int32).at[dest].set(flat_tok[order])
    gate_buf = jnp.zeros((npad,), x.dtype).at[dest].set(flat_gate[order].astype(x.dtype))
    block_e = jnp.minimum(jnp.searchsorted(pend, jnp.arange(n_blocks) * EXPERT_BLOCK, side="right"), N_EXPERTS - 1)
    x_pad = jnp.concatenate([x, jnp.zeros((1, d), x.dtype)], 0)
    xb = x_pad[tok_buf].reshape(n_blocks, EXPERT_BLOCK, d)
    yb = lax.map(lambda a: swiglu(a[1], w_eg[a[0]], w_eu[a[0]], w_ed[a[0]]), (block_e, xb))
    yb = yb.reshape(npad, d) * gate_buf[:, None]
    return jnp.zeros((t + 1, d), yb.dtype).at[tok_buf].add(yb)[:t]


def channel_mixer(h, w_router, router_bias, w_eg, w_eu, w_ed, w_sg, w_su, w_sd):
    b, l, d = h.shape
    x = h.reshape(b * l, d)
    idx, gate = route(x, w_router, router_bias)
    out = swiglu(x, w_sg, w_su, w_sd) + routed_experts(x, idx, gate, w_eg, w_eu, w_ed)
    return out.reshape(b, l, d)


def run_group(x, p, caches, prm):
    b = x.shape[0]
    new_k, new_v, new_conv, new_ssm = [], [], [], []
    for i in range(DEPTH):
        if caches is None:
            kv_past = None
            conv_past = jnp.zeros((b, SSD_CONV - 1, SSD_CONV_DIM), x.dtype)
            ssm_past = jnp.zeros((b, SSD_HEADS, SSD_HEAD_DIM, SSD_STATE), x.dtype)
        else:
            kv_past = (caches[0][i], caches[1][i])
            conv_past, ssm_past = caches[2][i], caches[3][i]
        mix, k, v, conv_s, ssm_s = token_mixers(
            x, prm["w_in"][i], prm["b_branch_gate"][i], prm["conv_w"][i], prm["conv_b"][i], prm["dt_bias"][i],
            prm["a_log"][i], prm["d_skip"][i], prm["ssd_norm_w"][i], prm["w_branch_a"][i], prm["w_branch_b"][i],
            prm["w_out"][i], kv_past, conv_past, ssm_past)
        h1 = layer_norm(DEEPNORM_ALPHA * x + mix, prm["ln1_g"][i], prm["ln1_b"][i])
        ffn = channel_mixer(h1, prm["w_router"][i], prm["router_bias"][i], prm["w_exp_gate"][i], prm["w_exp_up"][i],
                            prm["w_exp_down"][i], prm["w_sh_gate"][i], prm["w_sh_up"][i], prm["w_sh_down"][i])
        h2 = layer_norm(DEEPNORM_ALPHA * h1 + ffn, prm["ln2_g"][i], prm["ln2_b"][i])
        x = h2 + jax.nn.sigmoid(h2 @ prm["w_ple_gate"][i]) * (p[i] @ prm["w_ple"][i])
        new_k.append(k)
        new_v.append(v)
        new_conv.append(conv_s)
        new_ssm.append(ssm_s)
    return x, (jnp.stack(new_k), jnp.stack(new_v), jnp.stack(new_conv), jnp.stack(new_ssm))


def setup_inputs(seed: int = 0) -> dict:
    key = jax.random.key(seed)
    ks = iter(jax.random.split(key, 40))

    def nrm(shape, scale=1.0):
        return jax.random.normal(next(ks), shape, jnp.float32) * scale

    u = jax.random.uniform(next(ks), (DEPTH, SSD_HEADS), jnp.float32)
    dt0 = jnp.exp(u * (math.log(0.1) - math.log(0.001)) + math.log(0.001))
    dt_bias = dt0 + jnp.log(-jnp.expm1(-dt0))
    a_log = jnp.log(jax.random.uniform(next(ks), (DEPTH, SSD_HEADS), jnp.float32, minval=1.0, maxval=16.0))
    return {
        "x_prompt": nrm((BATCH, SEQ, D_MODEL)),
        "x_sample": nrm((DEC_BATCH, DEC_SEQ, D_MODEL)),
        "p_prompt": nrm((DEPTH, BATCH, SEQ, PLE_DIM)),
        "p_sample": nrm((DEPTH, DEC_BATCH, DEC_SEQ, PLE_DIM)),
        "cache_k": nrm((DEPTH, DEC_BATCH, PAST_LEN, SB_HEADS, SB_HEAD_DIM)),
        "cache_v": nrm((DEPTH, DEC_BATCH, PAST_LEN, SB_HEADS, SB_HEAD_DIM)),
        "state_conv": nrm((DEPTH, DEC_BATCH, SSD_CONV - 1, SSD_CONV_DIM)),
        "state_ssm": nrm((DEPTH, DEC_BATCH, SSD_HEADS, SSD_HEAD_DIM, SSD_STATE), 0.5),
        "w_in": nrm((DEPTH, D_MODEL, IN_DIM), D_MODEL ** -0.5),
        "b_branch_gate": nrm((DEPTH, 2 * D_MODEL), 0.02),
        "conv_w": nrm((DEPTH, SSD_CONV, SSD_CONV_DIM), SSD_CONV ** -0.5),
        "conv_b": nrm((DEPTH, SSD_CONV_DIM), 0.02),
        "dt_bias": dt_bias,
        "a_log": a_log,
        "d_skip": 1.0 + nrm((DEPTH, SSD_HEADS), 0.02),
        "ssd_norm_w": 1.0 + nrm((DEPTH, SSD_WIDTH), 0.02),
        "w_branch_a": nrm((DEPTH, SB_WIDTH, D_MODEL), SB_WIDTH ** -0.5 * DEEPNORM_BETA),
        "w_branch_b": nrm((DEPTH, SSD_WIDTH, D_MODEL), SSD_WIDTH ** -0.5 * DEEPNORM_BETA),
        "w_out": nrm((DEPTH, D_MODEL, D_MODEL), D_MODEL ** -0.5 * DEEPNORM_BETA),
        "ln1_g": 1.0 + nrm((DEPTH, D_MODEL), 0.02),
        "ln1_b": nrm((DEPTH, D_MODEL), 0.02),
        "w_router": nrm((DEPTH, D_MODEL, N_EXPERTS), D_MODEL ** -0.5),
        "router_bias": nrm((DEPTH, N_EXPERTS), 0.01),
        "w_exp_gate": nrm((DEPTH, N_EXPERTS, D_MODEL, EXPERT_FF), D_MODEL ** -0.5),
        "w_exp_up": nrm((DEPTH, N_EXPERTS, D_MODEL, EXPERT_FF), D_MODEL ** -0.5),
        "w_exp_down": nrm((DEPTH, N_EXPERTS, EXPERT_FF, D_MODEL), EXPERT_FF ** -0.5 * DEEPNORM_BETA),
        "w_sh_gate": nrm((DEPTH, D_MODEL, SHARED_FF), D_MODEL ** -0.5),
        "w_sh_up": nrm((DEPTH, D_MODEL, SHARED_FF), D_MODEL ** -0.5),
        "w_sh_down": nrm((DEPTH, SHARED_FF, D_MODEL), SHARED_FF ** -0.5 * DEEPNORM_BETA),
        "ln2_g": 1.0 + nrm((DEPTH, D_MODEL), 0.02),
        "ln2_b": nrm((DEPTH, D_MODEL), 0.02),
        "w_ple": nrm((DEPTH, PLE_DIM, D_MODEL), PLE_DIM ** -0.5),
        "w_ple_gate": nrm((DEPTH, D_MODEL, D_MODEL), D_MODEL ** -0.5),
    }


def reference(x_prompt, x_sample, p_prompt, p_sample, cache_k, cache_v, state_conv, state_ssm,
              w_in, b_branch_gate, conv_w, conv_b, dt_bias, a_log, d_skip, ssd_norm_w,
              w_branch_a, w_branch_b, w_out, ln1_g, ln1_b, w_router, router_bias,
              w_exp_gate, w_exp_up, w_exp_down, w_sh_gate, w_sh_up, w_sh_down,
              ln2_g, ln2_b, w_ple, w_ple_gate):
    prm = dict(w_in=w_in, b_branch_gate=b_branch_gate, conv_w=conv_w, conv_b=conv_b, dt_bias=dt_bias,
               a_log=a_log, d_skip=d_skip, ssd_norm_w=ssd_norm_w, w_branch_a=w_branch_a, w_branch_b=w_branch_b,
               w_out=w_out, ln1_g=ln1_g, ln1_b=ln1_b, w_router=w_router, router_bias=router_bias,
               w_exp_gate=w_exp_gate, w_exp_up=w_exp_up, w_exp_down=w_exp_down, w_sh_gate=w_sh_gate,
               w_sh_up=w_sh_up, w_sh_down=w_sh_down, ln2_g=ln2_g, ln2_b=ln2_b, w_ple=w_ple, w_ple_gate=w_ple_gate)
    y_prompt, (k_prompt, v_prompt, conv_prompt, ssm_prompt) = run_group(x_prompt, p_prompt, None, prm)
    y_sample, (k_sample, v_sample, conv_sample, ssm_sample) = run_group(
        x_sample, p_sample, (cache_k, cache_v, state_conv, state_ssm), prm)
    return (y_prompt, y_sample, k_prompt, v_prompt, conv_prompt, ssm_prompt, k_sample, v_sample, conv_sample, ssm_sample)
```

```python
import functools

import jax
import jax.numpy as jnp
from jax import lax
from jax.experimental import pallas as pl
from jax.experimental.pallas import tpu as pltpu

F32 = jnp.float32
BF16 = jnp.bfloat16

N_EXPERT_GROUPS = 8
TOPK_GROUPS = 4
TOP_K = 8
ROUTED_SCALE = 2.5
LN_EPS = 1e-5
RMS_EPS = 1e-5
SSD_CONV = 4

LOG_WEIGHT_FLOOR = -105.0

VMEM_LIMIT_BYTES = 56 * 1024 * 1024
MOE_ROW_BLOCK = 128


def _cparams(*sem):
    return pltpu.CompilerParams(dimension_semantics=sem, vmem_limit_bytes=VMEM_LIMIT_BYTES)


def _softplus(x):
    return jnp.maximum(x, 0.0) + jnp.log(1.0 + jnp.exp(-jnp.abs(x)))


def _sigmoid(x):
    return 1.0 / (1.0 + jnp.exp(-x))


def _silu(x):
    return x * _sigmoid(x)


def _split3(x):
    hi = x.astype(BF16)
    r1 = x - hi.astype(F32)
    mid = r1.astype(BF16)
    lo = (r1 - mid.astype(F32)).astype(BF16)
    return hi, mid, lo


def _dot(a, b):
    return jnp.dot(a, b, preferred_element_type=F32)


def _dot_nt(a, b):
    return lax.dot_general(a, b, (((1,), (1,)), ((), ())), preferred_element_type=F32)


def _dot_f32_rhs01(a, sel):
    hi, mid, lo = _split3(a)
    return _dot(hi, sel) + _dot(mid, sel) + _dot(lo, sel)


def _dot_f32_lhs01(sel, a):
    hi, mid, lo = _split3(a)
    return _dot(sel, hi) + _dot(sel, mid) + _dot(sel, lo)


def _mm_kernel(x_ref, w_ref, o_ref):
    o_ref[...] = _dot(x_ref[...], w_ref[...].astype(BF16)).astype(o_ref.dtype)


def _matmul(x, w, col0, ncols, tm, tn, out_dtype, name):
    m, k = x.shape
    assert m % tm == 0 and ncols % tn == 0 and col0 % tn == 0
    j0 = col0 // tn
    return pl.pallas_call(
        _mm_kernel,
        grid=(m // tm, ncols // tn),
        in_specs=[pl.BlockSpec((tm, k), lambda i, j: (i, 0)),
                  pl.BlockSpec((k, tn), lambda i, j: (0, j + j0))],
        out_specs=pl.BlockSpec((tm, tn), lambda i, j: (i, j)),
        out_shape=jax.ShapeDtypeStruct((m, ncols), out_dtype),
        compiler_params=_cparams("parallel", "arbitrary"),
        name=name,
    )(x, w)


def _mm3_kernel(x_ref, w_ref, o_ref):
    xh, xm, _ = _split3(x_ref[...])
    wh, wm, _ = _split3(w_ref[...])
    o_ref[...] = _dot(xh, wh) + (_dot(xh, wm) + _dot(xm, wh))


def _matmul_precise(x, w, tm, name):
    m, k = x.shape
    n = w.shape[1]
    return pl.pallas_call(
        _mm3_kernel,
        grid=(m // tm,),
        in_specs=[pl.BlockSpec((tm, k), lambda i: (i, 0)),
                  pl.BlockSpec((k, n), lambda i: (0, 0))],
        out_specs=pl.BlockSpec((tm, n), lambda i: (i, 0)),
        out_shape=jax.ShapeDtypeStruct((m, n), F32),
        compiler_params=_cparams("parallel"),
        name=name,
    )(x, w)


def _sb_block(q, k, v, carry, strict_lower):
    m, n = q.shape[0], k.shape[0]
    z = _dot_nt(q, k.astype(BF16))
    sp = _softplus(z)
    log_keep = -sp
    if strict_lower:
        reach = lax.broadcasted_iota(jnp.int32, (m, n), 1) < lax.broadcasted_iota(jnp.int32, (m, n), 0)
        log_keep = jnp.where(reach, log_keep, 0.0)
    after = (lax.broadcasted_iota(jnp.int32, (n, n), 0) > lax.broadcasted_iota(jnp.int32, (n, n), 1)).astype(BF16)
    hi = log_keep.astype(BF16)
    lo = (log_keep - hi.astype(F32)).astype(BF16)
    later = _dot(hi, after) + _dot(lo, after)
    w = jnp.exp((z - sp) + later + carry)
    if strict_lower:
        w = jnp.where(reach, w, 0.0)
    out = _dot(w.astype(BF16), v.astype(BF16))
    return out, carry + jnp.sum(log_keep, axis=1, keepdims=True)


def _sb_sweep(q, k_ref, v_ref, first_block, bk, acc_ref, carry_ref):
    def cond(state):
        kb, bound = state
        return jnp.logical_and(kb >= 0, bound > LOG_WEIGHT_FLOOR)

    def body(state):
        kb, _ = state
        start = pl.multiple_of(kb * bk, bk)
        out, carry = _sb_block(q, k_ref[pl.ds(start, bk), :], v_ref[pl.ds(start, bk), :], carry_ref[...], False)
        acc_ref[...] += out
        carry_ref[...] = carry
        return kb - 1, jnp.max(carry)

    lax.while_loop(cond, body, (first_block, jnp.max(carry_ref[...])))


def _attn_prompt_kernel(q_ref, k_ref, v_ref, o_ref, acc_ref, carry_ref, *, blk, scale):
    i = pl.program_id(1)
    q = (q_ref[...] * scale).astype(BF16)
    start = pl.multiple_of(i * blk, blk)
    out, carry = _sb_block(q, k_ref[pl.ds(start, blk), :], v_ref[pl.ds(start, blk), :],
                           jnp.zeros((blk, 1), F32), True)
    acc_ref[...] = out
    carry_ref[...] = carry
    _sb_sweep(q, k_ref, v_ref, i - 1, blk, acc_ref, carry_ref)
    o_ref[...] = acc_ref[...].astype(o_ref.dtype)


def _attn_prompt(q, k, v, seq, heads, hd, blk):
    kern = functools.partial(_attn_prompt_kernel, blk=blk, scale=hd ** -0.5)
    return pl.pallas_call(
        kern,
        grid=(heads, seq // blk),
        in_specs=[pl.BlockSpec((blk, hd), lambda h, i: (i, h)),
                  pl.BlockSpec((seq, hd), lambda h, i: (0, h)),
                  pl.BlockSpec((seq, hd), lambda h, i: (0, h))],
        out_specs=pl.BlockSpec((blk, hd), lambda h, i: (i, h)),
        out_shape=jax.ShapeDtypeStruct((seq, heads * hd), BF16),
        scratch_shapes=[pltpu.VMEM((blk, hd), F32), pltpu.VMEM((blk, 1), F32)],
        compiler_params=_cparams("parallel", "arbitrary"),
        name="sb_attn_prompt",
    )(q, k, v)


def _attn_sample_kernel(q_ref, kn_ref, vn_ref, kp_ref, vp_ref, o_ref, acc_ref, carry_ref, *, bk, scale):
    lq = q_ref.shape[0]
    q = (q_ref[...] * scale).astype(BF16)
    out, carry = _sb_block(q, kn_ref[...], vn_ref[...], jnp.zeros((lq, 1), F32), True)
    acc_ref[...] = out
    carry_ref[...] = carry
    _sb_sweep(q, kp_ref, vp_ref, kp_ref.shape[0] // bk - 1, bk, acc_ref, carry_ref)
    o_ref[...] = acc_ref[...].astype(o_ref.dtype)


def _attn_sample(q, k, v, cache_k, cache_v, row0, nb, lq, heads, hd, bk):
    past = cache_k.shape[1]
    assert row0 % lq == 0 and past % bk == 0
    r0 = row0 // lq
    kern = functools.partial(_attn_sample_kernel, bk=bk, scale=hd ** -0.5)
    new_spec = pl.BlockSpec((lq, hd), lambda b, h: (r0 + b, h))
    past_spec = pl.BlockSpec((None, past, hd), lambda b, h: (b, 0, h))
    return pl.pallas_call(
        kern,
        grid=(nb, heads),
        in_specs=[new_spec, new_spec, new_spec, past_spec, past_spec],
        out_specs=pl.BlockSpec((lq, hd), lambda b, h: (b, h)),
        out_shape=jax.ShapeDtypeStruct((nb * lq, heads * hd), BF16),
        scratch_shapes=[pltpu.VMEM((lq, hd), F32), pltpu.VMEM((lq, 1), F32)],
        compiler_params=_cparams("parallel", "arbitrary"),
        name="sb_attn_sample",
    )(q, k, v, cache_k, cache_v)


def _conv_silu(x_ref, past_ref, hist_ref, buf_ref, w_ref, b_ref, first):
    lc = x_ref.shape[0]

    @pl.when(first)
    def _():
        hist_ref[...] = jnp.zeros(hist_ref.shape, F32)
        hist_ref[5:8, :] = past_ref[...]

    buf_ref[0:8, :] = hist_ref[...]
    buf_ref[8:8 + lc, :] = x_ref[...]
    y = b_ref[...] + w_ref[3:4, :] * buf_ref[8:8 + lc, :]
    y = y + w_ref[2:3, :] * buf_ref[7:7 + lc, :]
    y = y + w_ref[1:2, :] * buf_ref[6:6 + lc, :]
    y = y + w_ref[0:1, :] * buf_ref[5:5 + lc, :]
    hist_ref[...] = buf_ref[lc:lc + 8, :]
    return _silu(y)


def _ssd_kernel(xs_ref, bm_ref, cm_ref, z_ref, dtc_ref, dtr_ref,
                pxs_ref, pbm_ref, pcm_ref, s0_ref,
                wxs_ref, wbm_ref, wcm_ref, bxs_ref, bbm_ref, bcm_ref,
                dtb_r_ref, dtb_c_ref, alog_r_ref, alog_c_ref, dskip_ref, nw_ref,
                y_ref, sout_ref,
                state_ref, hxs_ref, hbm_ref, hcm_ref, fxs_ref, fbm_ref, fcm_ref, *, hpg, hd):
    c = pl.program_id(2)
    last = pl.num_programs(2) - 1
    lc = xs_ref.shape[0]
    gw = hpg * hd
    first = c == 0

    @pl.when(first)
    def _():
        state_ref[...] = s0_ref[...]

    xs = _conv_silu(xs_ref, pxs_ref, hxs_ref, fxs_ref, wxs_ref, bxs_ref, first)
    bm = _conv_silu(bm_ref, pbm_ref, hbm_ref, fbm_ref, wbm_ref, bbm_ref, first)
    cm = _conv_silu(cm_ref, pcm_ref, hcm_ref, fcm_ref, wcm_ref, bcm_ref, first)

    dt_c = _softplus(dtc_ref[...] + dtb_r_ref[...])
    a_c = -jnp.exp(alog_r_ref[...]) * dt_c
    head_of_lane = lax.broadcasted_iota(jnp.int32, (lc, gw), 1) // hd
    dt_x = jnp.zeros((lc, gw), F32)
    a_x = jnp.zeros((lc, gw), F32)
    for r in range(hpg):
        dt_x = jnp.where(head_of_lane == r, dt_c[:, r:r + 1], dt_x)
        a_x = jnp.where(head_of_lane == r, a_c[:, r:r + 1], a_x)
    row = lax.broadcasted_iota(jnp.int32, (lc, lc), 0)
    col = lax.broadcasted_iota(jnp.int32, (lc, lc), 1)
    causal = row >= col
    acum_x = _dot_f32_lhs01(causal.astype(BF16), a_x)
    total_x = acum_x[lc - 1:lc, :]

    dt_r = _softplus(dtr_ref[...] + dtb_c_ref[...])
    acum_r = -jnp.exp(alog_c_ref[...]) * dt_r
    lanes = acum_r.shape[1]
    lane = lax.broadcasted_iota(jnp.int32, acum_r.shape, 1)
    shift = 1
    while shift < lanes:
        acum_r = acum_r + jnp.where(lane >= shift, pltpu.roll(acum_r, shift, 1), 0.0)
        shift *= 2
    total_xt = jnp.concatenate(
        [jnp.broadcast_to(acum_r[r:r + 1, lc - 1:lc], (hd, 1)) for r in range(hpg)], axis=0)

    xdt = xs * dt_x
    xdt_b = xdt.astype(BF16)
    cm_b = cm.astype(BF16)
    bm_b = bm.astype(BF16)
    cb = _dot_nt(cm_b, bm_b)

    y = jnp.zeros((lc, gw), F32)
    for r in range(hpg):
        seg = acum_x[:, r * hd:r * hd + 1] - acum_r[r:r + 1, 0:lc]
        scores = cb * jnp.exp(jnp.where(causal, seg, -jnp.inf))
        y = y + jnp.where(head_of_lane == r, _dot(scores.astype(BF16), xdt_b), 0.0)

    state = state_ref[...]
    y = y + _dot_nt(cm_b, state.astype(BF16)) * jnp.exp(acum_x)
    to_end = jnp.exp(total_x - acum_x)
    upd = lax.dot_general((xdt * to_end).astype(BF16), bm_b, (((0,), (0,)), ((), ())),
                          preferred_element_type=F32)
    state = state * jnp.exp(total_xt) + upd
    state_ref[...] = state

    @pl.when(c == last)
    def _():
        sout_ref[...] = state

    y = y + dskip_ref[...] * xs
    y = y * _silu(z_ref[...])
    y = y * lax.rsqrt(jnp.mean(y * y, axis=-1, keepdims=True) + RMS_EPS)
    y_ref[...] = (y * nw_ref[...]).astype(y_ref.dtype)


def _ssd(xbc, z, dt_raw, row0, nb, seq, lc, conv_past, state0, conv_w, conv_b, dt_bias, a_log, d_skip, norm_w,
         heads, hd, n_state, groups, name):
    width = heads * hd
    hpg = heads // groups
    gw = hpg * hd
    nc = seq // lc
    assert seq % lc == 0 and row0 % lc == 0 and gw % 128 == 0 and n_state % 128 == 0
    r0 = row0 // lc
    rows = nb * seq
    nbo = width // n_state
    gbo = width // gw

    dt_rows = lax.slice_in_dim(dt_raw, row0, row0 + rows, axis=0)
    lanes = -(-lc // 128) * 128
    dt_c = dt_rows.reshape(nb * nc, lc, groups, hpg).transpose(2, 0, 1, 3)
    dt_r = jnp.pad(dt_c.transpose(0, 1, 3, 2), ((0, 0), (0, 0), (0, 0), (0, lanes - lc)))
    dtb_r = dt_bias.reshape(groups, 1, hpg)
    dtb_c = dt_bias.reshape(groups, hpg, 1)
    alog_r = a_log.reshape(groups, 1, hpg)
    alog_c = a_log.reshape(groups, hpg, 1)
    dskip_x = jnp.repeat(d_skip, hd).reshape(1, width)
    nw = norm_w.reshape(1, width)
    cb2 = conv_b.reshape(1, -1)
    s0 = state0.reshape(nb, groups, gw, n_state)

    tok = lambda b, g, c: r0 + b * nc + c
    in_specs = [
        pl.BlockSpec((lc, gw), lambda b, g, c: (tok(b, g, c), g)),
        pl.BlockSpec((lc, n_state), lambda b, g, c: (tok(b, g, c), nbo + g)),
        pl.BlockSpec((lc, n_state), lambda b, g, c: (tok(b, g, c), nbo + groups + g)),
        pl.BlockSpec((lc, gw), lambda b, g, c: (tok(b, g, c), g)),
        pl.BlockSpec((None, None, lc, hpg), lambda b, g, c: (g, b * nc + c, 0, 0)),
        pl.BlockSpec((None, None, hpg, lanes), lambda b, g, c: (g, b * nc + c, 0, 0)),
        pl.BlockSpec((None, SSD_CONV - 1, gw), lambda b, g, c: (b, 0, g)),
        pl.BlockSpec((None, SSD_CONV - 1, n_state), lambda b, g, c: (b, 0, nbo + g)),
        pl.BlockSpec((None, SSD_CONV - 1, n_state), lambda b, g, c: (b, 0, nbo + groups + g)),
        pl.BlockSpec((None, None, gw, n_state), lambda b, g, c: (b, g, 0, 0)),
        pl.BlockSpec((SSD_CONV, gw), lambda b, g, c: (0, g)),
        pl.BlockSpec((SSD_CONV, n_state), lambda b, g, c: (0, nbo + g)),
        pl.BlockSpec((SSD_CONV, n_state), lambda b, g, c: (0, nbo + groups + g)),
        pl.BlockSpec((1, gw), lambda b, g, c: (0, g)),
        pl.BlockSpec((1, n_state), lambda b, g, c: (0, nbo + g)),
        pl.BlockSpec((1, n_state), lambda b, g, c: (0, nbo + groups + g)),
        pl.BlockSpec((None, 1, hpg), lambda b, g, c: (g, 0, 0)),
        pl.BlockSpec((None, hpg, 1), lambda b, g, c: (g, 0, 0)),
        pl.BlockSpec((None, 1, hpg), lambda b, g, c: (g, 0, 0)),
        pl.BlockSpec((None, hpg, 1), lambda b, g, c: (g, 0, 0)),
        pl.BlockSpec((1, gw), lambda b, g, c: (0, g)),
        pl.BlockSpec((1, gw), lambda b, g, c: (0, g)),
    ]
    out_specs = [
        pl.BlockSpec((lc, gw), lambda b, g, c: (b * nc + c, g)),
        pl.BlockSpec((None, None, gw, n_state), lambda b, g, c: (b, g, 0, 0)),
    ]
    scratch = [
        pltpu.VMEM((gw, n_state), F32),
        pltpu.VMEM((8, gw), F32), pltpu.VMEM((8, n_state), F32), pltpu.VMEM((8, n_state), F32),
        pltpu.VMEM((lc + 8, gw), F32), pltpu.VMEM((lc + 8, n_state), F32), pltpu.VMEM((lc + 8, n_state), F32),
    ]
    y, s_out = pl.pallas_call(
        functools.partial(_ssd_kernel, hpg=hpg, hd=hd),
        grid=(nb, groups, nc),
        in_specs=in_specs,
        out_specs=out_specs,
        out_shape=[jax.ShapeDtypeStruct((rows, width), BF16),
                   jax.ShapeDtypeStruct((nb, groups, gw, n_state), F32)],
        scratch_shapes=scratch,
        compiler_params=_cparams("parallel", "parallel", "arbitrary"),
        name=name,
    )(xbc, xbc, xbc, z, dt_c, dt_r, conv_past, conv_past, conv_past, s0,
      conv_w, conv_w, conv_w, cb2, cb2, cb2, dtb_r, dtb_c, alog_r, alog_c, dskip_x, nw)
    return y, s_out.reshape(nb, heads, hd, n_state)


def _merge_kernel(attn_ref, y_ref, wa_ref, wb_ref, ga_ref, gb_ref, ba_ref, bb_ref, o_ref):
    pa = _dot(attn_ref[...], wa_ref[...])
    pb = _dot(y_ref[...], wb_ref[...])
    ga = _sigmoid(ga_ref[...] + ba_ref[...])
    gb = _sigmoid(gb_ref[...] + bb_ref[...])
    o_ref[...] = (ga * pa + gb * pb).astype(o_ref.dtype)


def _merge(attn, y, w_pa, w_pb, gates, b_gate, tm, tn):
    t, wa = attn.shape
    wy = y.shape[1]
    d = w_pa.shape[1]
    nj = d // tn
    return pl.pallas_call(
        _merge_kernel,
        grid=(t // tm, nj),
        in_specs=[pl.BlockSpec((tm, wa), lambda i, j: (i, 0)),
                  pl.BlockSpec((tm, wy), lambda i, j: (i, 0)),
                  pl.BlockSpec((wa, tn), lambda i, j: (0, j)),
                  pl.BlockSpec((wy, tn), lambda i, j: (0, j)),
                  pl.BlockSpec((tm, tn), lambda i, j: (i, j)),
                  pl.BlockSpec((tm, tn), lambda i, j: (i, j + nj)),
                  pl.BlockSpec((1, tn), lambda i, j: (0, j)),
                  pl.BlockSpec((1, tn), lambda i, j: (0, j + nj))],
        out_specs=pl.BlockSpec((tm, tn), lambda i, j: (i, j)),
        out_shape=jax.ShapeDtypeStruct((t, d), BF16),
        compiler_params=_cparams("parallel", "arbitrary"),
        name="branch_merge",
    )(attn, y, w_pa, w_pb, gates, gates, b_gate, b_gate)


def _layer_norm(v, g, b):
    mu = jnp.mean(v, axis=-1, keepdims=True)
    vc = v - mu
    var = jnp.mean(vc * vc, axis=-1, keepdims=True)
    return vc * lax.rsqrt(var + LN_EPS) * g + b


def _outproj_ln_kernel(m_ref, w_ref, x_ref, g_ref, b_ref, h_ref, hb_ref, *, alpha):
    mix = _dot(m_ref[...], w_ref[...])
    h = _layer_norm(alpha * x_ref[...] + mix, g_ref[...], b_ref[...])
    h_ref[...] = h
    hb_ref[...] = h.astype(BF16)


def _outproj_ln(merged, w_o, x, g, b, alpha, tm):
    t, d = x.shape
    row = pl.BlockSpec((tm, d), lambda i: (i, 0))
    vec = pl.BlockSpec((1, d), lambda i: (0, 0))
    return pl.pallas_call(
        functools.partial(_outproj_ln_kernel, alpha=alpha),
        grid=(t // tm,),
        in_specs=[row, pl.BlockSpec((d, d), lambda i: (0, 0)), row, vec, vec],
        out_specs=[row, row],
        out_shape=[jax.ShapeDtypeStruct((t, d), F32), jax.ShapeDtypeStruct((t, d), BF16)],
        compiler_params=_cparams("parallel"),
        name="outproj_ln1",
    )(merged, w_o, x, g, b)


def _router_kernel(h_ref, wt_ref, bias_ref, idx_ref, gate_ref, *, n_groups, topk_groups, top_k, scale):
    ne = wt_ref.shape[0]
    tm = h_ref.shape[0]
    per = ne // n_groups
    hh, hm, _ = _split3(h_ref[...])
    wh, wm, _ = _split3(wt_ref[...])
    logits = _dot_nt(wh, hh) + (_dot_nt(wh, hm) + _dot_nt(wm, hh))
    scores = _sigmoid(logits)
    choice = scores + bias_ref[...]
    neg = -jnp.inf

    grouped = choice.reshape(n_groups, per, tm)
    e_in_g = lax.broadcasted_iota(jnp.int32, (n_groups, per, tm), 1).astype(F32)
    m1 = jnp.max(grouped, axis=1, keepdims=True)
    i1 = jnp.min(jnp.where(grouped == m1, e_in_g, float(per)), axis=1, keepdims=True)
    m2 = jnp.max(jnp.where(e_in_g == i1, neg, grouped), axis=1, keepdims=True)
    gscore = m1 + m2

    g_iota = lax.broadcasted_iota(jnp.int32, (n_groups, 1, tm), 0).astype(F32)
    g_of_e = lax.broadcasted_iota(jnp.int32, (n_groups, per, tm), 0).astype(F32)
    masked = jnp.full((n_groups, per, tm), neg, F32)
    for _ in range(topk_groups):
        m = jnp.max(gscore, axis=0, keepdims=True)
        gi = jnp.min(jnp.where(gscore == m, g_iota, float(n_groups)), axis=0, keepdims=True)
        masked = jnp.where(g_of_e == gi, grouped, masked)
        gscore = jnp.where(g_iota == gi, neg, gscore)
    masked = masked.reshape(ne, tm)

    e_iota = lax.broadcasted_iota(jnp.int32, (ne, tm), 0).astype(F32)
    k_iota = lax.broadcasted_iota(jnp.int32, (top_k, tm), 0)
    idx = jnp.zeros((top_k, tm), F32)
    w = jnp.zeros((top_k, tm), F32)
    for j in range(top_k):
        m = jnp.max(masked, axis=0, keepdims=True)
        ei = jnp.min(jnp.where(masked == m, e_iota, float(ne)), axis=0, keepdims=True)
        hit = e_iota == ei
        wj = jnp.sum(jnp.where(hit, scores, 0.0), axis=0, keepdims=True)
        idx = jnp.where(k_iota == j, ei, idx)
        w = jnp.where(k_iota == j, wj, w)
        masked = jnp.where(hit, neg, masked)
    idx_ref[...] = idx.astype(jnp.int32)
    gate_ref[...] = w / jnp.sum(w, axis=0, keepdims=True) * scale


def _router(h, w_router_t, router_bias, tm):
    t, d = h.shape
    ne = w_router_t.shape[0]
    kern = functools.partial(_router_kernel, n_groups=N_EXPERT_GROUPS, topk_groups=TOPK_GROUPS,
                             top_k=TOP_K, scale=ROUTED_SCALE)
    return pl.pallas_call(
        kern,
        grid=(t // tm,),
        in_specs=[pl.BlockSpec((tm, d), lambda i: (i, 0)),
                  pl.BlockSpec((ne, d), lambda i: (0, 0)),
                  pl.BlockSpec((ne, 1), lambda i: (0, 0))],
        out_specs=[pl.BlockSpec((TOP_K, tm), lambda i: (0, i)), pl.BlockSpec((TOP_K, tm), lambda i: (0, i))],
        out_shape=[jax.ShapeDtypeStruct((TOP_K, t), jnp.int32), jax.ShapeDtypeStruct((TOP_K, t), F32)],
        compiler_params=_cparams("parallel"),
        name="router",
    )(h, w_router_t, router_bias.reshape(ne, 1))


def _swiglu_kernel(x_ref, wg_ref, wu_ref, wd_ref, o_ref):
    x = x_ref[...]
    hmid = _silu(_dot(x, wg_ref[...])) * _dot(x, wu_ref[...])
    o_ref[...] = _dot(hmid.astype(BF16), wd_ref[...])


def _shared_expert(x, wg, wu, wd, tm):
    t, d = x.shape
    ff = wg.shape[1]
    return pl.pallas_call(
        _swiglu_kernel,
        grid=(t // tm,),
        in_specs=[pl.BlockSpec((tm, d), lambda i: (i, 0)),
                  pl.BlockSpec((d, ff), lambda i: (0, 0)),
                  pl.BlockSpec((d, ff), lambda i: (0, 0)),
                  pl.BlockSpec((ff, d), lambda i: (0, 0))],
        out_specs=pl.BlockSpec((tm, d), lambda i: (i, 0)),
        out_shape=jax.ShapeDtypeStruct((t, d), F32),
        compiler_params=_cparams("parallel"),
        name="shared_expert",
    )(x, wg, wu, wd)


def _experts_kernel(be_ref, nused_ref, x_ref, gate_ref, wg_ref, wu_ref, wd_ref, o_ref, wgb_ref, wub_ref, wdb_ref):
    b = pl.program_id(0)
    prev = be_ref[jnp.maximum(b - 1, 0)]
    fresh = jnp.logical_or(b == 0, be_ref[b] != prev)
    active = b < nused_ref[0]

    @pl.when(jnp.logical_and(active, fresh))
    def _():
        wgb_ref[...] = wg_ref[...].astype(BF16)
        wub_ref[...] = wu_ref[...].astype(BF16)
        wdb_ref[...] = wd_ref[...].astype(BF16)

    @pl.when(active)
    def _():
        x = x_ref[...]
        hmid = _silu(_dot(x, wgb_ref[...])) * _dot(x, wub_ref[...])
        o_ref[...] = _dot(hmid.astype(BF16), wdb_ref[...]) * gate_ref[...]


def _routed_experts(xs, gate_buf, block_e, n_used, w_eg, w_eu, w_ed, bm):
    npad, d = xs.shape
    ne, _, ff = w_eg.shape
    nblk = npad // bm

    def row_map(b, be, nu):
        return (jnp.minimum(b, nu[0] - 1), 0)

    def w_map(b, be, nu):
        return (be[b], 0, 0)

    grid_spec = pltpu.PrefetchScalarGridSpec(
        num_scalar_prefetch=2,
        grid=(nblk,),
        in_specs=[pl.BlockSpec((bm, d), row_map),
                  pl.BlockSpec((bm, 1), row_map),
                  pl.BlockSpec((None, d, ff), w_map),
                  pl.BlockSpec((None, d, ff), w_map),
                  pl.BlockSpec((None, ff, d), w_map)],
        out_specs=pl.BlockSpec((bm, d), row_map),
        scratch_shapes=[pltpu.VMEM((d, ff), BF16), pltpu.VMEM((d, ff), BF16), pltpu.VMEM((ff, d), BF16)],
    )
    return pl.pallas_call(
        _experts_kernel,
        grid_spec=grid_spec,
        out_shape=jax.ShapeDtypeStruct((npad, d), F32),
        compiler_params=_cparams("arbitrary"),
        name="routed_experts",
    )(block_e, n_used, xs, gate_buf, w_eg, w_eu, w_ed)


def _final_kernel(sh_ref, rt_ref, h1_ref, p_ref, g_ref, b_ref, wpg_ref, wp_ref, o_ref, *, alpha):
    ffn = sh_ref[...] + rt_ref[...]
    h2 = _layer_norm(alpha * h1_ref[...] + ffn, g_ref[...], b_ref[...])
    gate = _sigmoid(_dot(h2.astype(BF16), wpg_ref[...]))
    o_ref[...] = h2 + gate * _dot(p_ref[...], wp_ref[...])


def _final(shared, routed, h1, p, g, b, w_pg, w_p, alpha, tm):
    t, d = h1.shape
    pd = p.shape[1]
    row = pl.BlockSpec((tm, d), lambda i: (i, 0))
    vec = pl.BlockSpec((1, d), lambda i: (0, 0))
    return pl.pallas_call(
        functools.partial(_final_kernel, alpha=alpha),
        grid=(t // tm,),
        in_specs=[row, row, row, pl.BlockSpec((tm, pd), lambda i: (i, 0)), vec, vec,
                  pl.BlockSpec((d, d), lambda i: (0, 0)), pl.BlockSpec((pd, d), lambda i: (0, 0))],
        out_specs=row,
        out_shape=jax.ShapeDtypeStruct((t, d), F32),
        compiler_params=_cparams("parallel"),
        name="ln2_ple",
    )(shared, routed, h1, p, g, b, w_pg, w_p)


def _dispatch_plan(idx_t, gate_t, ne, bm):
    top_k, t = idx_t.shape
    n = top_k * t
    flat_e = idx_t.reshape(n)
    order = jnp.argsort(flat_e)
    e_sorted = flat_e[order]
    counts = jnp.zeros((ne,), jnp.int32).at[flat_e].add(1)
    padded = (counts + bm - 1) // bm * bm
    pend = jnp.cumsum(padded)
    pstart = pend - padded
    start = jnp.cumsum(counts) - counts
    slot_sorted = pstart[e_sorted] + jnp.arange(n, dtype=jnp.int32) - start[e_sorted]
    nblk = (n + ne * (bm - 1) + bm - 1) // bm
    npad = nblk * bm
    tok_buf = jnp.zeros((npad,), jnp.int32).at[slot_sorted].set((order % t).astype(jnp.int32))
    gate_buf = jnp.zeros((npad,), F32).at[slot_sorted].set(gate_t.reshape(n)[order])
    slot = jnp.zeros((n,), jnp.int32).at[order].set(slot_sorted).reshape(top_k, t)
    block_e = jnp.minimum(jnp.searchsorted(pend, jnp.arange(nblk, dtype=jnp.int32) * bm, side="right"),
                          ne - 1).astype(jnp.int32)
    n_used = (pend[-1] // bm).astype(jnp.int32).reshape(1)
    return tok_buf, gate_buf.reshape(npad, 1), slot, block_e, n_used


def _largest_tile(m, cap):
    tile = cap
    while m % tile:
        tile //= 2
    return tile


def kernel(x_prompt, x_sample, p_prompt, p_sample, cache_k, cache_v, state_conv, state_ssm, w_in, b_branch_gate,
           conv_w, conv_b, dt_bias, a_log, d_skip, ssd_norm_w, w_branch_a, w_branch_b, w_out, ln1_g, ln1_b,
           w_router, router_bias, w_exp_gate, w_exp_up, w_exp_down, w_sh_gate, w_sh_up, w_sh_down,
           ln2_g, ln2_b, w_ple, w_ple_gate):
    depth = w_in.shape[0]
    assert depth == 1 and x_prompt.shape[0] == 1
    alpha = (2 * depth) ** 0.25
    _, seq, d = x_prompt.shape
    nb, lq, _ = x_sample.shape
    past, sb_heads, sb_hd = cache_k.shape[2:]
    ssd_heads, ssd_hd, n_state = state_ssm.shape[2:]
    conv_dim = conv_w.shape[2]
    sb_w = sb_heads * sb_hd
    ssd_w = ssd_heads * ssd_hd
    groups = (conv_dim - ssd_w) // (2 * n_state)
    ne = w_router.shape[2]
    t = seq + nb * lq
    tm = _largest_tile(t, 512)

    x = jnp.concatenate([x_prompt.reshape(seq, d), x_sample.reshape(nb * lq, d)], axis=0)
    xb = x.astype(BF16)
    p = jnp.concatenate([p_prompt.reshape(seq, -1), p_sample.reshape(nb * lq, -1)], axis=0).astype(BF16)

    w_in0 = w_in[0].astype(BF16)
    o_z = 3 * sb_w
    o_xbc = o_z + ssd_w
    o_dt = o_xbc + conv_dim
    o_gate = o_dt + ssd_heads
    tn = 512
    q = _matmul(xb, w_in0, 0, sb_w, tm, tn, F32, "proj_q")
    k = _matmul(xb, w_in0, sb_w, sb_w, tm, tn, F32, "proj_k")
    v = _matmul(xb, w_in0, 2 * sb_w, sb_w, tm, tn, F32, "proj_v")
    z = _matmul(xb, w_in0, o_z, ssd_w, tm, tn, F32, "proj_z")
    xbc = _matmul(xb, w_in0, o_xbc, conv_dim, tm, tn, F32, "proj_xbc")
    w_gate = lax.slice_in_dim(w_in0, o_gate, o_gate + 2 * d, axis=1)
    gates = _matmul(xb, w_gate, 0, 2 * d, tm, tn, F32, "proj_gates")
    dt_raw = _matmul_precise(x, lax.slice_in_dim(w_in[0], o_dt, o_gate, axis=1), tm, "proj_dt")

    attn_p = _attn_prompt(q, k, v, seq, sb_heads, sb_hd, 128)
    attn_s = _attn_sample(q, k, v, cache_k[0].reshape(nb, past, sb_w), cache_v[0].reshape(nb, past, sb_w),
                          seq, nb, lq, sb_heads, sb_hd, 128)
    attn = jnp.concatenate([attn_p, attn_s], axis=0)

    ssd_args = (conv_w[0], conv_b[0], dt_bias[0], a_log[0], d_skip[0], ssd_norm_w[0],
                ssd_heads, ssd_hd, n_state, groups)
    y_p, ssm_p = _ssd(xbc, z, dt_raw, 0, 1, seq, 128, jnp.zeros((1, SSD_CONV - 1, conv_dim), F32),
                      jnp.zeros((1, ssd_heads, ssd_hd, n_state), F32), *ssd_args, name="ssd_prompt")
    y_s, ssm_s = _ssd(xbc, z, dt_raw, seq, nb, lq, lq, state_conv[0], state_ssm[0], *ssd_args, name="ssd_sample")
    y = jnp.concatenate([y_p, y_s], axis=0)

    merged = _merge(attn, y, w_branch_a[0].astype(BF16), w_branch_b[0].astype(BF16), gates,
                    b_branch_gate[0].reshape(1, -1), tm, 512)
    h1, h1b = _outproj_ln(merged, w_out[0].astype(BF16), x, ln1_g[0].reshape(1, d), ln1_b[0].reshape(1, d),
                          alpha, 256)

    idx_t, gate_t = _router(h1, w_router[0].T, router_bias[0], 256)
    shared = _shared_expert(h1b, w_sh_gate[0].astype(BF16), w_sh_up[0].astype(BF16), w_sh_down[0].astype(BF16), tm)
    tok_buf, gate_buf, slot, block_e, n_used = _dispatch_plan(idx_t, gate_t, ne, MOE_ROW_BLOCK)
    xs = jnp.take(h1b, tok_buf, axis=0)
    yb = _routed_experts(xs, gate_buf, block_e, n_used, w_exp_gate[0], w_exp_up[0], w_exp_down[0], MOE_ROW_BLOCK)
    routed = jnp.sum(jnp.take(yb, slot, axis=0), axis=0)
    out = _final(shared, routed, h1, p, ln2_g[0].reshape(1, d), ln2_b[0].reshape(1, d),
                 w_ple_gate[0].astype(BF16), w_ple[0].astype(BF16), alpha, 256)

    y_prompt = out[:seq].reshape(1, seq, d)
    y_sample = out[seq:].reshape(nb, lq, d)
    kv_shape_p = (1, 1, seq, sb_heads, sb_hd)
    kv_shape_s = (1, nb, lq, sb_heads, sb_hd)
    xbc_s = xbc[seq:].reshape(nb, lq, conv_dim)
    return (y_prompt, y_sample,
            k[:seq].reshape(kv_shape_p), v[:seq].reshape(kv_shape_p),
            xbc[seq - (SSD_CONV - 1):seq].reshape(1, 1, SSD_CONV - 1, conv_dim),
            ssm_p.reshape(1, 1, ssd_heads, ssd_hd, n_state),
            k[seq:].reshape(kv_shape_s), v[seq:].reshape(kv_shape_s),
            xbc_s[:, lq - (SSD_CONV - 1):].reshape(1, nb, SSD_CONV - 1, conv_dim),
            ssm_s.reshape(1, nb, ssd_heads, ssd_hd, n_state))
```
